```python
import jax
import jax.numpy as jnp
from jax import lax
import numpy as np

D_MODEL = 1024
BATCH = 2
SEQ = 16384
DEPTH = 2

GRID_W = 64
CTX_LEN = 256
HEAD_DIM = 64
Q_BLOCK = 128
ROPE_THETA = 10000.0
NORM_EPS = 1e-6

GQA_HEADS = 8
GQA_KV_HEADS = 2
NA_HEADS = 4
NA_WIN_R = 8
NA_WIN_C = 16
RWKV_HEADS = 4
RWKV_N = 64
RWKV_W = RWKV_HEADS * RWKV_N
RWKV_DECAY_LORA = 32
RWKV_AAA_LORA = 32
RWKV_GATE_LORA = 64
RWKV_GN_EPS = 64e-5
MLA_HEADS = 4
MLA_Q_LORA = 384
MLA_KV_LORA = 256
MLA_NOPE = 64
MLA_ROPE = 32
MLA_V = 64
N_BRANCH = 4
GQA_W = GQA_HEADS * HEAD_DIM
NA_W = NA_HEADS * HEAD_DIM
MLA_W = MLA_HEADS * MLA_V
BRANCH_WIDTHS = (GQA_W, NA_W, RWKV_W, MLA_W)
MIX_W = GQA_W + NA_W + RWKV_W + MLA_W
IN_SPLITS = (GQA_W, GQA_KV_HEADS * HEAD_DIM, GQA_KV_HEADS * HEAD_DIM, NA_W, NA_W, NA_W,
             RWKV_W, RWKV_W, RWKV_W, MLA_Q_LORA, MLA_KV_LORA, MLA_ROPE, N_BRANCH * D_MODEL)
ALL_PARTS = (0, 1, 2, 3, 4, 5, 6, 7, 8, 9, 10, 11, 12)
KV_PARTS = (1, 2, 4, 5, 7, 8, 10, 11)
P_IN = GQA_W + 4 * GQA_KV_HEADS * HEAD_DIM // 2 + 3 * NA_W + 3 * RWKV_W + MLA_Q_LORA + MLA_KV_LORA + MLA_ROPE + N_BRANCH * D_MODEL
DENSE_FF = 2816
N_EXPERTS = 8
TOP_K = 2
EXPERT_FF = 1408
N_DENSE = (DEPTH + 1) // 2
N_MOE = DEPTH // 2

kernel_name = 'hybrid_gated_branch_diffusion_trunk'


def _rms_norm(x, g):
    xf = x.astype(jnp.float32)
    y = xf * lax.rsqrt(jnp.mean(xf * xf, axis=-1, keepdims=True) + NORM_EPS)
    return (y * g.astype(jnp.float32)).astype(x.dtype)


def _modulate(h, shift, scale):
    return h * (1 + scale) + shift


def _split_heads(t, n):
    b, n_tok, _ = t.shape
    return t.reshape(b, n_tok, n, -1).transpose(0, 2, 1, 3)


def _merge_heads(t):
    b, n, n_tok, d = t.shape
    return t.transpose(0, 2, 1, 3).reshape(b, n_tok, n * d)


def _axial_rope_tables(n_tok, rot_dim):
    t = jnp.arange(n_tok)
    row = (t // GRID_W).astype(jnp.float32)
    col = (t % GRID_W).astype(jnp.float32)
    n_pairs = rot_dim // 4
    inv_freq = ROPE_THETA ** (-jnp.arange(n_pairs, dtype=jnp.float32) / n_pairs)
    ang = jnp.concatenate([row[:, None] * inv_freq, col[:, None] * inv_freq], axis=-1)
    return jnp.cos(ang), jnp.sin(ang)


def _apply_rope(x, cos, sin):
    half = x.shape[-1] // 2
    x1 = x[..., :half].astype(jnp.float32)
    x2 = x[..., half:].astype(jnp.float32)
    return jnp.concatenate([x1 * cos - x2 * sin, x1 * sin + x2 * cos], axis=-1).astype(x.dtype)


def _block_attention(q, k, v, scale):
    b, hq, n_q, dq = q.shape
    hkv, dv = k.shape[1], v.shape[-1]
    g = hq // hkv
    nb = n_q // Q_BLOCK
    qb = q.reshape(b, hkv, g, nb, Q_BLOCK, dq).transpose(3, 0, 1, 2, 4, 5)

    def one_block(qi):
        s = jnp.einsum('bkgqd,bkld->bkgql', qi, k).astype(jnp.float32) * scale
        p = jax.nn.softmax(s, axis=-1).astype(v.dtype)
        return jnp.einsum('bkgql,bkld->bkgqd', p, v)

    o = lax.map(one_block, qb)
    return o.transpose(1, 2, 3, 0, 4, 5).reshape(b, hq, n_q, dv)


def _in_project(hx, w_in, keep):
    offs = [0]
    for s in IN_SPLITS:
        offs.append(offs[-1] + s)
    if len(keep) == len(IN_SPLITS):
        z = hx @ w_in
        return [z[..., offs[i]:offs[i + 1]] for i in range(len(IN_SPLITS))]
    z = hx @ jnp.concatenate([w_in[:, offs[i]:offs[i + 1]] for i in keep], axis=1)
    parts = [None] * len(IN_SPLITS)
    pos = 0
    for i in keep:
        parts[i] = z[..., pos:pos + IN_SPLITS[i]]
        pos += IN_SPLITS[i]
    return parts


def _gqa_mixer(q, k, v, qc, kc, vc, q_gain, k_gain, cos, sin, ctx_out):
    scale = HEAD_DIM ** -0.5
    q = _apply_rope(_rms_norm(_split_heads(q, GQA_HEADS), q_gain), cos, sin)
    k = _apply_rope(_rms_norm(_split_heads(k, GQA_KV_HEADS), k_gain), cos, sin)
    v = _split_heads(v, GQA_KV_HEADS)
    kc = _rms_norm(_split_heads(kc, GQA_KV_HEADS), k_gain)
    vc = _split_heads(vc, GQA_KV_HEADS)
    y = _block_attention(q, jnp.concatenate([kc, k], axis=2), jnp.concatenate([vc, v], axis=2), scale)
    yc = None
    if ctx_out:
        yc = _merge_heads(_block_attention(_rms_norm(_split_heads(qc, GQA_HEADS), q_gain), kc, vc, scale))
    return _merge_heads(y), yc


def _na_mixer(q, k, v, qc, kc, vc, rpb, rows, ctx_out):
    b, n_lat, _ = q.shape
    wr = min(NA_WIN_R, rows)
    scale = HEAD_DIM ** -0.5

    def grid(t):
        return _split_heads(t, NA_HEADS).reshape(b, NA_HEADS, rows, GRID_W, HEAD_DIM)

    qg, kg, vg = grid(q), grid(k), grid(v)
    kc = _split_heads(kc, NA_HEADS)
    vc = _split_heads(vc, NA_HEADS)
    r_idx = jnp.arange(rows)
    c_idx = jnp.arange(GRID_W)
    key_rows = jnp.clip(r_idx - wr // 2, 0, rows - wr)[:, None] + jnp.arange(wr)[None, :]
    col_start = jnp.clip(c_idx - NA_WIN_C // 2, 0, GRID_W - NA_WIN_C)
    col_in = (c_idx[None, :] >= col_start[:, None]) & (c_idx[None, :] < col_start[:, None] + NA_WIN_C)
    dr = key_rows - r_idx[:, None] + (NA_WIN_R - 1)
    dc = jnp.clip(c_idx[None, :] - c_idx[:, None], 1 - NA_WIN_C, NA_WIN_C - 1) + (NA_WIN_C - 1)
    bias = rpb[:, dr[:, None, :, None], dc[None, :, None, :]].astype(jnp.float32)
    k_rows = kg[:, :, key_rows]
    v_rows = vg[:, :, key_rows]
    s_win = jnp.einsum('bhrqd,bhrnkd->bhrqnk', qg, k_rows).astype(jnp.float32) * scale + bias
    s_win = jnp.where(col_in[:, None, :], s_win, -jnp.inf)
    s_ctx = jnp.einsum('bhrqd,bhcd->bhrqc', qg, kc).astype(jnp.float32) * scale
    n_win = wr * GRID_W
    s = jnp.concatenate([s_win.reshape(b, NA_HEADS, rows, GRID_W, n_win), s_ctx], axis=-1)
    p = jax.nn.softmax(s, axis=-1).astype(v.dtype)
    p_win = p[..., :n_win].reshape(b, NA_HEADS, rows, GRID_W, wr, GRID_W)
    o = (jnp.einsum('bhrqnk,bhrnkd->bhrqd', p_win, v_rows)
         + jnp.einsum('bhrqc,bhcd->bhrqd', p[..., n_win:], vc))
    y = _merge_heads(o.reshape(b, NA_HEADS, n_lat, HEAD_DIM))
    yc = None
    if ctx_out:
        yc = _merge_heads(_block_attention(_split_heads(qc, NA_HEADS), kc, vc, scale))
    return y, yc


def _token_shift(t):
    p = jnp.pad(t, ((0, 0), (1, 1), (0, 0)))
    return 0.5 * (p[:, :-2] + p[:, 2:]) - t


def _unit_heads(t):
    b, n_tok, w = t.shape
    th = t.reshape(b, n_tok, RWKV_HEADS, RWKV_N).astype(jnp.float32)
    th = th * lax.rsqrt(jnp.maximum(jnp.sum(th * th, axis=-1, keepdims=True), 1e-12))
    return th.reshape(b, n_tok, w).astype(t.dtype)


def _rwkv_prepare(h, r0, k0, v0, rw, with_out):
    mu_x, mu_p, w0, w1, w2, a0, a1, a2, g1, g2, k_k, k_a = rw[:12]
    dh = _token_shift(h)
    xw = h + dh * mu_x[0]
    xa = h + dh * mu_x[1]
    k = k0 + _token_shift(k0) * mu_p[1]
    v = v0 + _token_shift(v0) * mu_p[2]
    kk = _unit_heads(k * k_k)
    per_dir = []
    for d in range(2):
        w_log = -jax.nn.softplus(-(w0[d] + jnp.tanh(xw @ w1[d]) @ w2[d])) - 0.5
        decay = jnp.exp(-jnp.exp(w_log.astype(jnp.float32)))
        a = jax.nn.sigmoid(a0[d] + (xa @ a1[d]) @ a2[d])
        per_dir.append((decay, k * (1 + (a - 1) * k_a), kk * a))
    if not with_out:
        return v, kk, per_dir, None, None
    r = r0 + _token_shift(r0) * mu_p[0]
    xg = h + dh * mu_x[2]
    gates = [jax.nn.sigmoid(xg @ g1[d]) @ g2[d] for d in range(2)]
    return v, kk, per_dir, r, gates


def _rwkv_scan(s0, decay, k, v, kk, b, r, reverse):
    def to_tm(t):
        return jnp.moveaxis(t.reshape(t.shape[0], t.shape[1], RWKV_HEADS, RWKV_N).astype(jnp.float32), 1, 0)

    xs = [to_tm(decay), to_tm(k), to_tm(v), to_tm(kk), to_tm(b)]
    if r is not None:
        xs.append(to_tm(r))

    def step(s, inp):
        w_t, k_t, v_t, kk_t, b_t = inp[:5]
        s_kk = jnp.einsum('bhvk,bhk->bhv', s, kk_t)
        s = s * w_t[:, :, None, :] - s_kk[..., None] * b_t[:, :, None, :] + v_t[..., None] * k_t[:, :, None, :]
        out = jnp.einsum('bhvk,bhk->bhv', s, inp[5]) if len(inp) > 5 else None
        return s, out

    s_fin, ys = lax.scan(step, s0, tuple(xs), reverse=reverse)
    if r is None:
        return s_fin, None
    return s_fin, jnp.moveaxis(ys, 0, 1)


def _rwkv_readout(o, r, k, v, g, r_k, ln_w, ln_b):
    b, n_tok = o.shape[:2]
    mu = jnp.mean(o, axis=-1, keepdims=True)
    var = jnp.mean(jnp.square(o - mu), axis=-1, keepdims=True)
    on = ((o - mu) * lax.rsqrt(var + RWKV_GN_EPS)).reshape(b, n_tok, RWKV_W) * ln_w + ln_b

    def hs(t):
        return t.reshape(b, n_tok, RWKV_HEADS, RWKV_N)

    bonus = (jnp.sum(hs(r) * hs(k) * r_k, axis=-1, keepdims=True) * hs(v)).reshape(b, n_tok, RWKV_W)
    return (on.astype(r.dtype) + bonus) * g


def _rwkv_mixer(h, r0, k0, v0, hc, rc0, kc0, vc0, rw, ctx_out):
    r_k, ln_w, ln_b = rw[12:]
    s0 = jnp.zeros((h.shape[0], RWKV_HEADS, RWKV_N, RWKV_N), jnp.float32)
    v, kk, dirs, r, g = _rwkv_prepare(h, r0, k0, v0, rw, True)
    vc, kkc, dirs_c, rc, gc = _rwkv_prepare(hc, rc0, kc0, vc0, rw, ctx_out)
    y = None
    yc = None
    for d in range(2):
        rev = d == 1
        dec_c, kd_c, b_c = dirs_c[d]
        s_ctx, oc = _rwkv_scan(s0, dec_c, kd_c, vc, kkc, b_c, rc, rev)
        dec, kd, bb = dirs[d]
        _, o = _rwkv_scan(s_ctx, dec, kd, v, kk, bb, r, rev)
        yd = _rwkv_readout(o, r, kd, v, g[d], r_k, ln_w, ln_b)
        y = yd if y is None else y + yd
        if ctx_out:
            ycd = _rwkv_readout(oc, rc, kd_c, vc, gc[d], r_k, ln_w, ln_b)
            yc = ycd if yc is None else yc + ycd
    return y, yc


def _mla_queries(cq, q_gain, w_uq):
    qh = _split_heads(_rms_norm(cq, q_gain) @ w_uq, MLA_HEADS)
    return qh[..., :MLA_NOPE], qh[..., MLA_NOPE:]


def _mla_keys_values(ckv, kr, kv_gain, w_ukv):
    kvh = _split_heads(_rms_norm(ckv, kv_gain) @ w_ukv, MLA_HEADS)
    return kvh[..., :MLA_NOPE], kvh[..., MLA_NOPE:], kr[:, None]


def _mla_mixer(cq, ckv, kr, cqc, ckvc, krc, q_gain, w_uq, kv_gain, w_ukv, cos, sin, ctx_out):
    def join(k_nope, k_rope):
        return jnp.concatenate([k_nope, jnp.broadcast_to(k_rope, k_nope.shape[:-1] + (MLA_ROPE,))], axis=-1)

    scale = (MLA_NOPE + MLA_ROPE) ** -0.5
    q_n, q_r = _mla_queries(cq, q_gain, w_uq)
    q = jnp.concatenate([q_n, _apply_rope(q_r, cos, sin)], axis=-1)
    k_n, v, k_r = _mla_keys_values(ckv, kr, kv_gain, w_ukv)
    k = join(k_n, _apply_rope(k_r, cos, sin))
    kc_n, vc, kc_r = _mla_keys_values(ckvc, krc, kv_gain, w_ukv)
    kc = join(kc_n, kc_r)
    y = _merge_heads(_block_attention(q, jnp.concatenate([kc, k], axis=2), jnp.concatenate([vc, v], axis=2), scale))
    yc = None
    if ctx_out:
        qc_n, qc_r = _mla_queries(cqc, q_gain, w_uq)
        yc = _merge_heads(_block_attention(jnp.concatenate([qc_n, qc_r], axis=-1), kc, vc, scale))
    return y, yc


def _gated_merge(branches, gate_logits, w_branch, w_out):
    b, n_tok, _ = gate_logits.shape
    gates = jax.nn.sigmoid(gate_logits.reshape(b, n_tok, N_BRANCH, D_MODEL))
    merged = None
    off = 0
    for i, yb in enumerate(branches):
        wd = BRANCH_WIDTHS[i]
        term = gates[:, :, i] * (yb @ w_branch[off:off + wd])
        merged = term if merged is None else merged + term
        off += wd
    return merged @ w_out


def _token_mixing(h, hc, rows, rope_h, rope_m, w_in, q_gain, k_gain, rpb, rw, mla_p, w_branch, w_out, ctx_out):
    z = _in_project(h, w_in, ALL_PARTS)
    zc = _in_project(hc, w_in, ALL_PARTS if ctx_out else KV_PARTS)
    ya, yca = _gqa_mixer(z[0], z[1], z[2], zc[0], zc[1], zc[2], q_gain, k_gain, rope_h[0], rope_h[1], ctx_out)
    yb, ycb = _na_mixer(z[3], z[4], z[5], zc[3], zc[4], zc[5], rpb, rows, ctx_out)
    yr, ycr = _rwkv_mixer(h, z[6], z[7], z[8], hc, zc[6], zc[7], zc[8], rw, ctx_out)
    ym, ycm = _mla_mixer(z[9], z[10], z[11], zc[9], zc[10], zc[11], mla_p[0], mla_p[1], mla_p[2], mla_p[3],
                         rope_m[0], rope_m[1], ctx_out)
    y = _gated_merge([ya, yb, yr, ym], z[12], w_branch, w_out)
    yc = _gated_merge([yca, ycb, ycr, ycm], zc[12], w_branch, w_out) if ctx_out else None
    return y, yc


def _swiglu(h, wg, wu, wd):
    return (jax.nn.silu(h @ wg) * (h @ wu)) @ wd


def _moe(h, w_router, b_router, wg, wu, wd):
    logits = (h @ w_router).astype(jnp.float32) + b_router.astype(jnp.float32)
    probs = jax.nn.softmax(logits, axis=-1)
    top_p, top_i = lax.top_k(probs, TOP_K)
    top_p = top_p / jnp.sum(top_p, axis=-1, keepdims=True)
    combine = jnp.sum(jax.nn.one_hot(top_i, N_EXPERTS, dtype=jnp.float32) * top_p[..., None], axis=-2)
    out = None
    for e in range(N_EXPERTS):
        term = combine[..., e:e + 1].astype(h.dtype) * _swiglu(h, wg[e], wu[e], wd[e])
        out = term if out is None else out + term
    return out


def _channel_mixer(hh, l, ffn_w_gate, ffn_w_up, ffn_w_down, moe_w_router, moe_b_router, moe_w_gate, moe_w_up, moe_w_down):
    i = l // 2
    if l % 2 == 0:
        return _swiglu(hh, ffn_w_gate[i], ffn_w_up[i], ffn_w_down[i])
    return _moe(hh, moe_w_router[i], moe_b_router[i], moe_w_gate[i], moe_w_up[i], moe_w_down[i])


def setup_inputs(seed: int = 0) -> dict:
    key = jax.random.key(seed)
    ks = iter(list(jax.random.split(key, 64)))
    D = D_MODEL

    def nrm(shape, s):
        return jax.random.normal(next(ks), shape, jnp.float32) * s

    def uni(shape, lo, hi):
        return jax.random.uniform(next(ks), shape, jnp.float32, lo, hi)

    def gain(shape):
        return 1.0 + nrm(shape, 0.05)

    return {
        'x': nrm((BATCH, SEQ, D), 1.0),
        'c': nrm((BATCH, D), 1.0),
        'ctx': nrm((BATCH, CTX_LEN, D), 1.0),
        'c_ctx': nrm((D,), 1.0),
        'w_mod': nrm((DEPTH, D, 6 * D), 0.5 * D ** -0.5),
        'b_mod': nrm((DEPTH, 6 * D), 0.02),
        'norm_mix_pre': gain((DEPTH, D)),
        'norm_mix_post': gain((DEPTH, D)),
        'norm_ffn_pre': gain((DEPTH, D)),
        'norm_ffn_post': gain((DEPTH, D)),
        'w_in': nrm((DEPTH, D, P_IN), D ** -0.5),
        'gqa_q_norm': gain((DEPTH, HEAD_DIM)),
        'gqa_k_norm': gain((DEPTH, HEAD_DIM)),
        'na_rpb': nrm((DEPTH, NA_HEADS, 2 * NA_WIN_R - 1, 2 * NA_WIN_C - 1), 0.1),
        'rwkv_mu_x': uni((DEPTH, 3, D), 0.0, 1.0),
        'rwkv_mu_p': uni((DEPTH, 3, RWKV_W), 0.0, 1.0),
        'rwkv_w0': uni((DEPTH, 2, RWKV_W), -6.0, -1.0),
        'rwkv_w1': nrm((DEPTH, 2, D, RWKV_DECAY_LORA), D ** -0.5),
        'rwkv_w2': nrm((DEPTH, 2, RWKV_DECAY_LORA, RWKV_W), 0.5 * RWKV_DECAY_LORA ** -0.5),
        'rwkv_a0': nrm((DEPTH, 2, RWKV_W), 0.5),
        'rwkv_a1': nrm((DEPTH, 2, D, RWKV_AAA_LORA), D ** -0.5),
        'rwkv_a2': nrm((DEPTH, 2, RWKV_AAA_LORA, RWKV_W), 0.5 * RWKV_AAA_LORA ** -0.5),
        'rwkv_g1': nrm((DEPTH, 2, D, RWKV_GATE_LORA), D ** -0.5),
        'rwkv_g2': nrm((DEPTH, 2, RWKV_GATE_LORA, RWKV_W), RWKV_GATE_LORA ** -0.5),
        'rwkv_k_k': 0.85 + nrm((DEPTH, RWKV_W), 0.05),
        'rwkv_k_a': 1.0 + nrm((DEPTH, RWKV_W), 0.05),
        'rwkv_r_k': nrm((DEPTH, RWKV_HEADS, RWKV_N), 0.1),
        'rwkv_ln_w': gain((DEPTH, RWKV_W)),
        'rwkv_ln_b': nrm((DEPTH, RWKV_W), 0.02),
        'mla_q_norm': gain((DEPTH, MLA_Q_LORA)),
        'mla_w_uq': nrm((DEPTH, MLA_Q_LORA, MLA_HEADS * (MLA_NOPE + MLA_ROPE)), MLA_Q_LORA ** -0.5),
        'mla_kv_norm': gain((DEPTH, MLA_KV_LORA)),
        'mla_w_ukv': nrm((DEPTH, MLA_KV_LORA, MLA_HEADS * (MLA_NOPE + MLA_V)), MLA_KV_LORA ** -0.5),
        'w_branch': nrm((DEPTH, MIX_W, D), NA_W ** -0.5),
        'w_out': nrm((DEPTH, D, D), D ** -0.5),
        'ffn_w_gate': nrm((N_DENSE, D, DENSE_FF), D ** -0.5),
        'ffn_w_up': nrm((N_DENSE, D, DENSE_FF), D ** -0.5),
        'ffn_w_down': nrm((N_DENSE, DENSE_FF, D), DENSE_FF ** -0.5),
        'moe_w_router': nrm((N_MOE, D, N_EXPERTS), D ** -0.5),
        'moe_b_router': nrm((N_MOE, N_EXPERTS), 0.01),
        'moe_w_gate': nrm((N_MOE, N_EXPERTS, D, EXPERT_FF), D ** -0.5),
        'moe_w_up': nrm((N_MOE, N_EXPERTS, D, EXPERT_FF), D ** -0.5),
        'moe_w_down': nrm((N_MOE, N_EXPERTS, EXPERT_FF, D), EXPERT_FF ** -0.5),
    }


def reference(x, c, ctx, c_ctx, w_mod, b_mod, norm_mix_pre, norm_mix_post, norm_ffn_pre, norm_ffn_post,
              w_in, gqa_q_norm, gqa_k_norm, na_rpb,
              rwkv_mu_x, rwkv_mu_p, rwkv_w0, rwkv_w1, rwkv_w2, rwkv_a0, rwkv_a1, rwkv_a2,
              rwkv_g1, rwkv_g2, rwkv_k_k, rwkv_k_a, rwkv_r_k, rwkv_ln_w, rwkv_ln_b,
              mla_q_norm, mla_w_uq, mla_kv_norm, mla_w_ukv, w_branch, w_out,
              ffn_w_gate, ffn_w_up, ffn_w_down,
              moe_w_router, moe_b_router, moe_w_gate, moe_w_up, moe_w_down):
    n_lat = x.shape[1]
    rows = n_lat // GRID_W
    rope_h = _axial_rope_tables(n_lat, HEAD_DIM)
    rope_m = _axial_rope_tables(n_lat, MLA_ROPE)
    silu_c = jax.nn.silu(c)
    silu_cc = jax.nn.silu(c_ctx)
    xc = ctx
    for l in range(DEPTH):
        ctx_out = l < DEPTH - 1
        m = jnp.split((silu_c @ w_mod[l] + b_mod[l])[:, None, :], 6, axis=-1)
        mc = jnp.split(silu_cc @ w_mod[l] + b_mod[l], 6, axis=-1)
        rw = (rwkv_mu_x[l], rwkv_mu_p[l], rwkv_w0[l], rwkv_w1[l], rwkv_w2[l], rwkv_a0[l], rwkv_a1[l], rwkv_a2[l],
              rwkv_g1[l], rwkv_g2[l], rwkv_k_k[l], rwkv_k_a[l], rwkv_r_k[l], rwkv_ln_w[l], rwkv_ln_b[l])
        mla_p = (mla_q_norm[l], mla_w_uq[l], mla_kv_norm[l], mla_w_ukv[l])
        h = _modulate(_rms_norm(x, norm_mix_pre[l]), m[0], m[1])
        hc = _modulate(_rms_norm(xc, norm_mix_pre[l]), mc[0], mc[1])
        y, yc = _token_mixing(h, hc, rows, rope_h, rope_m, w_in[l], gqa_q_norm[l], gqa_k_norm[l], na_rpb[l],
                              rw, mla_p, w_branch[l], w_out[l], ctx_out)
        x = x + m[2] * _rms_norm(y, norm_mix_post[l])
        h = _modulate(_rms_norm(x, norm_ffn_pre[l]), m[3], m[4])
        x = x + m[5] * _rms_norm(_channel_mixer(h, l, ffn_w_gate, ffn_w_up, ffn_w_down, moe_w_router, moe_b_router,
                                                moe_w_gate, moe_w_up, moe_w_down), norm_ffn_post[l])
        if ctx_out:
            xc = xc + mc[2] * _rms_norm(yc, norm_mix_post[l])
            hc = _modulate(_rms_norm(xc, norm_ffn_pre[l]), mc[3], mc[4])
            xc = xc + mc[5] * _rms_norm(_channel_mixer(hc, l, ffn_w_gate, ffn_w_up, ffn_w_down, moe_w_router,
                                                      moe_b_router, moe_w_gate, moe_w_up, moe_w_down),
                                       norm_ffn_post[l])
    return x
```

```python
import functools

import jax
import jax.numpy as jnp
from jax import lax
from jax.experimental import pallas as pl
from jax.experimental.pallas import tpu as pltpu

GRID_W = 64
HEAD_DIM = 64
ROPE_THETA = 10000.0
NORM_EPS = 1e-6
GQA_HEADS = 8
GQA_KV_HEADS = 2
NA_HEADS = 4
NA_WIN_R = 8
NA_WIN_C = 16
RWKV_HEADS = 4
RWKV_N = 64
RWKV_W = RWKV_HEADS * RWKV_N
RWKV_GN_EPS = 64e-5
MLA_HEADS = 4
MLA_Q_LORA = 384
MLA_KV_LORA = 256
MLA_NOPE = 64
MLA_ROPE = 32
MLA_V = 64
N_EXPERTS = 8
GQA_W = GQA_HEADS * HEAD_DIM
GQA_KV_W = GQA_KV_HEADS * HEAD_DIM
NA_W = NA_HEADS * HEAD_DIM
MLA_W = MLA_HEADS * MLA_V
SCAN_CHUNK = 64
LANES = 128

VMEM_LIMIT_BYTES = 56 * 1024 * 1024

F32 = jnp.float32
BF16 = jnp.bfloat16
HI = lax.Precision.HIGHEST


def _pick(n, cands):
    for c in cands:
        if n % c == 0:
            return c
    raise ValueError(f"no tile in {cands} divides {n}")


def _cparams(*sem):
    return pltpu.CompilerParams(dimension_semantics=sem, vmem_limit_bytes=VMEM_LIMIT_BYTES)


def _dot(a, b, precision=None):
    return jnp.dot(a, b, preferred_element_type=F32, precision=precision)


def _dot_nt(a, b, precision=None):
    return lax.dot_general(a, b, (((1,), (1,)), ((), ())), preferred_element_type=F32, precision=precision)


def _dot_tn(a, b, precision=None):
    return lax.dot_general(a, b, (((0,), (0,)), ((), ())), preferred_element_type=F32, precision=precision)


def _rms(x, g):
    return x * lax.rsqrt(jnp.mean(x * x, axis=-1, keepdims=True) + NORM_EPS) * g


def _sigmoid(x):
    return 1.0 / (1.0 + jnp.exp(-x))


def _silu(x):
    return x * _sigmoid(x)


def _const_spec(shape, single_buffer=False):
    nd = len(shape)
    if single_buffer:
        return pl.BlockSpec(shape, lambda *_: (0,) * nd, pipeline_mode=pl.Buffered(1))
    return pl.BlockSpec(shape, lambda *_: (0,) * nd)


def _seg_mod(mods_ref, row, is_ctx):
    return jnp.where(is_ctx, mods_ref[0, 8 + row:9 + row, :], mods_ref[0, row:row + 1, :])


def _is_ctx_rows(tile_rows, tile_idx, tok_off, n_ctx):
    gidx = lax.broadcasted_iota(jnp.int32, (tile_rows, 1), 0) + tile_idx * tile_rows + tok_off
    return gidx < n_ctx


def _adaln_kernel(c_ref, w_ref, b_ref, o_ref):
    s = _silu(c_ref[...]).astype(BF16)
    o_ref[0] = _dot(s, w_ref[0].astype(BF16)) + b_ref[0]


def _adaln(c8, w_mod, b_mod):
    depth, d, n6 = w_mod.shape
    tn = _pick(n6, (1536, 1024, 512, 128))
    return pl.pallas_call(
        _adaln_kernel,
        out_shape=jax.ShapeDtypeStruct((depth, 8, n6), F32),
        grid=(depth, n6 // tn),
        in_specs=[
            pl.BlockSpec((8, d), lambda l, j: (0, 0)),
            pl.BlockSpec((1, d, tn), lambda l, j: (l, 0, j)),
            pl.BlockSpec((1, 1, tn), lambda l, j: (l, 0, j)),
        ],
        out_specs=pl.BlockSpec((1, 8, tn), lambda l, j: (l, 0, j)),
        compiler_params=_cparams("arbitrary", "arbitrary"),
        name="adaln",
    )(c8, w_mod, b_mod.reshape(depth, 1, n6))


_IN_GROUPS = (
    ("gq", GQA_W), ("gqs", GQA_W), ("gk", GQA_KV_W), ("gks", GQA_KV_W), ("gv", GQA_KV_W),
    ("nq", NA_W), ("nk", NA_W), ("nv", NA_W),
    ("zs", 3 * RWKV_W + 256), ("zl", 256),
    ("cq", MLA_Q_LORA), ("ckv", MLA_KV_LORA), ("kr", LANES),
    ("gate", 4 * 1024),
)


def _in_offsets():
    offs, o = {}, 0
    for name, w in _IN_GROUPS:
        offs[name] = (o, w)
        o += w
    return offs, o


def _inproj_kernel(x_ref, mods_ref, gain_ref, w_ref, hg_ref, wuq_ref, wukv_ref, mg_ref,
                   ch_ref, sh_ref, cm_ref, sm_ref,
                   gq_ref, gk_ref, gv_ref, nq_ref, nk_ref, nv_ref, zs_ref, zl_ref,
                   mqn_ref, mqr_ref, mkn_ref, mkr_ref, mv_ref, gate_ref, *, n_ctx):
    tm = x_ref.shape[1]
    offs, _ = _in_offsets()
    is_ctx = _is_ctx_rows(tm, pl.program_id(1), 0, n_ctx)
    h = _rms(x_ref[0], gain_ref[...])
    h = h * (1.0 + _seg_mod(mods_ref, 1, is_ctx)) + _seg_mod(mods_ref, 0, is_ctx)
    hb = h.astype(BF16)

    def proj(name):
        o, w = offs[name]
        return _dot(hb, w_ref[:, o:o + w])

    ch = ch_ref[...]
    sh = sh_ref[...]
    scale = HEAD_DIM ** -0.5

    def norm_rope_heads(z, zsw, gain, gain_sw, n_heads, out_ref, mult):
        for j in range(n_heads):
            zj = z[:, j * HEAD_DIM:(j + 1) * HEAD_DIM]
            zsj = zsw[:, j * HEAD_DIM:(j + 1) * HEAD_DIM]
            rinv = lax.rsqrt(jnp.mean(zj * zj, axis=-1, keepdims=True) + NORM_EPS)
            out_ref[0, j] = ((zj * gain * ch + zsj * gain_sw * sh) * (rinv * mult)).astype(out_ref.dtype)

    norm_rope_heads(proj("gq"), proj("gqs"), hg_ref[0:1, :], hg_ref[1:2, :], GQA_HEADS, gq_ref, scale)
    norm_rope_heads(proj("gk"), proj("gks"), hg_ref[2:3, :], hg_ref[3:4, :], GQA_KV_HEADS, gk_ref, 1.0)

    def split_heads(z, n_heads, out_ref, mult):
        for j in range(n_heads):
            out_ref[0, j] = (z[:, j * HEAD_DIM:(j + 1) * HEAD_DIM] * mult).astype(out_ref.dtype)

    split_heads(proj("gv"), GQA_KV_HEADS, gv_ref, 1.0)
    split_heads(proj("nq"), NA_HEADS, nq_ref, scale)
    split_heads(proj("nk"), NA_HEADS, nk_ref, 1.0)
    split_heads(proj("nv"), NA_HEADS, nv_ref, 1.0)
    zs_ref[0] = proj("zs")
    zl_ref[0] = proj("zl")
    gate_ref[0] = proj("gate")

    cm = cm_ref[...]
    sm = sm_ref[...]
    mscale = (MLA_NOPE + MLA_ROPE) ** -0.5
    cq = _rms(proj("cq"), mg_ref[0:1, :]).astype(BF16)
    qn = _dot(cq, wuq_ref[:, 0:256])
    qr = _dot(cq, wuq_ref[:, 256:384])
    qrs = _dot(cq, wuq_ref[:, 384:512])
    for j in range(MLA_HEADS):
        mqn_ref[0, j] = (qn[:, j * MLA_NOPE:(j + 1) * MLA_NOPE] * mscale).astype(BF16)
        a = qr[:, j * MLA_ROPE:(j + 1) * MLA_ROPE]
        b = qrs[:, j * MLA_ROPE:(j + 1) * MLA_ROPE]
        mqr_ref[0, j] = ((a * cm + b * sm) * mscale).astype(BF16)
    ckv = _rms(proj("ckv"), mg_ref[1:2, 0:MLA_KV_LORA]).astype(BF16)
    kn = _dot(ckv, wukv_ref[:, 0:256])
    vv = _dot(ckv, wukv_ref[:, 256:512])
    for j in range(MLA_HEADS):
        mkn_ref[0, j] = kn[:, j * MLA_NOPE:(j + 1) * MLA_NOPE].astype(BF16)
        mv_ref[0, j] = vv[:, j * MLA_V:(j + 1) * MLA_V].astype(BF16)
    kr = proj("kr")
    mkr_ref[0] = (kr[:, 0:MLA_ROPE] * cm + kr[:, MLA_ROPE:2 * MLA_ROPE] * sm).astype(BF16)


def _swap_halves_cols(w, width):
    shp = w.shape
    g = w.reshape(shp[:-1] + (shp[-1] // width, 2, width // 2))
    return jnp.flip(g, axis=-2).reshape(shp)


def _inproj(x, mods, gain, w_all, hgains, wuq, wukv, mgains, rope, n_ctx):
    b, n, d = x.shape
    tm = _pick(n, (256, 128))
    _, ctot = _in_offsets()
    ch, sh, cm, sm = rope
    hm = lambda nh, w: jax.ShapeDtypeStruct((b, nh, n, w), BF16)
    hspec = lambda nh, w: pl.BlockSpec((1, nh, tm, w), lambda bi, i: (bi, 0, i, 0))
    tspec = lambda w: pl.BlockSpec((1, tm, w), lambda bi, i: (bi, i, 0))
    rspec = lambda w: pl.BlockSpec((tm, w), lambda bi, i: (i, 0))
    out_shape = [
        hm(GQA_HEADS, 64), hm(GQA_KV_HEADS, 64), hm(GQA_KV_HEADS, 64),
        hm(NA_HEADS, 64), hm(NA_HEADS, 64), hm(NA_HEADS, 64),
        jax.ShapeDtypeStruct((b, n, 1024), F32), jax.ShapeDtypeStruct((b, n, 256), F32),
        hm(MLA_HEADS, MLA_NOPE), hm(MLA_HEADS, MLA_ROPE), hm(MLA_HEADS, MLA_NOPE),
        jax.ShapeDtypeStruct((b, n, MLA_ROPE), BF16), hm(MLA_HEADS, MLA_V),
        jax.ShapeDtypeStruct((b, n, 4096), F32),
    ]
    out_specs = [
        hspec(GQA_HEADS, 64), hspec(GQA_KV_HEADS, 64), hspec(GQA_KV_HEADS, 64),
        hspec(NA_HEADS, 64), hspec(NA_HEADS, 64), hspec(NA_HEADS, 64),
        tspec(1024), tspec(256),
        hspec(MLA_HEADS, MLA_NOPE), hspec(MLA_HEADS, MLA_ROPE), hspec(MLA_HEADS, MLA_NOPE),
        tspec(MLA_ROPE), hspec(MLA_HEADS, MLA_V),
        tspec(4096),
    ]
    return pl.pallas_call(
        functools.partial(_inproj_kernel, n_ctx=n_ctx),
        out_shape=out_shape,
        grid=(b, n // tm),
        in_specs=[
            tspec(d),
            pl.BlockSpec((1, 16, d), lambda bi, i: (bi, 0, 0)),
            _const_spec((1, d)),
            _const_spec((d, ctot), single_buffer=True),
            _const_spec((8, 64)),
            _const_spec(wuq.shape),
            _const_spec(wukv.shape),
            _const_spec(mgains.shape),
            rspec(64), rspec(64), rspec(MLA_ROPE), rspec(MLA_ROPE),
        ],
        out_specs=out_specs,
        compiler_params=_cparams("arbitrary", "arbitrary"),
        name="inproj",
    )(x, mods, gain, w_all, hgains, wuq, wukv, mgains, ch, sh, cm, sm)


def _flash_kernel(q_ref, k_ref, v_ref, o_ref, m_sc, l_sc, acc_sc, *, tk, nk):
    g, tq, dq = q_ref.shape[1:]
    dv = v_ref.shape[-1]
    q = q_ref[0].reshape(g * tq, dq)
    m_sc[...] = jnp.full(m_sc.shape, -jnp.inf, F32)
    l_sc[...] = jnp.zeros(l_sc.shape, F32)
    acc_sc[...] = jnp.zeros(acc_sc.shape, F32)

    def body(j, carry):
        off = pl.multiple_of(j * tk, tk)
        k = k_ref[0, 0, pl.ds(off, tk), :]
        v = v_ref[0, 0, pl.ds(off, tk), :]
        s = _dot_nt(q, k)
        m_prev = m_sc[...]
        m_new = jnp.maximum(m_prev, jnp.max(s, axis=-1, keepdims=True))
        p = jnp.exp(s - m_new)
        alpha = jnp.exp(m_prev - m_new)
        l_sc[...] = alpha * l_sc[...] + jnp.sum(p, axis=-1, keepdims=True)
        acc_sc[...] = alpha * acc_sc[...] + _dot(p.astype(BF16), v)
        m_sc[...] = m_new
        return carry

    lax.fori_loop(0, nk, body, 0)
    o = acc_sc[...] / l_sc[...]
    o_ref[0] = o.reshape(g, tq, dv).astype(o_ref.dtype)


def _flash(q, k, v, *, q_off, n_q, n_kv, tq, tk):
    b, hq, _, dq = q.shape
    hkv = k.shape[1]
    dv = v.shape[-1]
    g = hq // hkv
    assert q_off % tq == 0 and n_q % tq == 0 and n_kv % tk == 0
    qo = q_off // tq
    return pl.pallas_call(
        functools.partial(_flash_kernel, tk=tk, nk=n_kv // tk),
        out_shape=jax.ShapeDtypeStruct((b, hq, n_q, dv), BF16),
        grid=(b, hkv, n_q // tq),
        in_specs=[
            pl.BlockSpec((1, g, tq, dq), lambda bi, hi, i: (bi, hi, i + qo, 0)),
            pl.BlockSpec((1, 1, n_kv, dq), lambda bi, hi, i: (bi, hi, 0, 0)),
            pl.BlockSpec((1, 1, n_kv, dv), lambda bi, hi, i: (bi, hi, 0, 0)),
        ],
        out_specs=pl.BlockSpec((1, g, tq, dv), lambda bi, hi, i: (bi, hi, i, 0)),
        scratch_shapes=[
            pltpu.VMEM((g * tq, 1), F32),
            pltpu.VMEM((g * tq, 1), F32),
            pltpu.VMEM((g * tq, dv), F32),
        ],
        compiler_params=_cparams("arbitrary", "arbitrary", "arbitrary"),
        name="flash",
    )(q, k, v)


def _na_kernel(q_ref, k_ref, v_ref, tab_ref, o_ref, *, n_ctx, rows, rows_per_step):
    i = pl.program_id(2)
    kc = k_ref[0, 0, 0:n_ctx, :]
    vc = v_ref[0, 0, 0:n_ctx, :]
    n_win = NA_WIN_R * GRID_W
    for rr in range(rows_per_step):
        r = i * rows_per_step + rr
        start = jnp.clip(r - NA_WIN_R // 2, 0, rows - NA_WIN_R)
        q = q_ref[0, 0, rr * GRID_W:(rr + 1) * GRID_W, :]
        off = pl.multiple_of(n_ctx + start * GRID_W, GRID_W)
        kw = k_ref[0, 0, pl.ds(off, n_win), :]
        vw = v_ref[0, 0, pl.ds(off, n_win), :]
        s_w = _dot_nt(q, kw) + tab_ref[0, r - start]
        s_c = _dot_nt(q, kc)
        m = jnp.maximum(jnp.max(s_w, axis=-1, keepdims=True), jnp.max(s_c, axis=-1, keepdims=True))
        p_w = jnp.exp(s_w - m)
        p_c = jnp.exp(s_c - m)
        l = jnp.sum(p_w, axis=-1, keepdims=True) + jnp.sum(p_c, axis=-1, keepdims=True)
        o = (_dot(p_w.astype(BF16), vw) + _dot(p_c.astype(BF16), vc)) / l
        o_ref[0, 0, rr * GRID_W:(rr + 1) * GRID_W, :] = o.astype(o_ref.dtype)


def _na(q, k, v, tab, n_ctx):
    b, h, n, dh = q.shape
    rows = (n - n_ctx) // GRID_W
    assert rows >= NA_WIN_R
    rows_per_step = 4
    tq = rows_per_step * GRID_W
    assert n_ctx % tq == 0 and (n - n_ctx) % tq == 0
    qo = n_ctx // tq
    return pl.pallas_call(
        functools.partial(_na_kernel, n_ctx=n_ctx, rows=rows, rows_per_step=rows_per_step),
        out_shape=jax.ShapeDtypeStruct((b, h, n - n_ctx, dh), BF16),
        grid=(b, h, rows // rows_per_step),
        in_specs=[
            pl.BlockSpec((1, 1, tq, dh), lambda bi, hi, i: (bi, hi, i + qo, 0)),
            pl.BlockSpec((1, 1, n, dh), lambda bi, hi, i: (bi, hi, 0, 0)),
            pl.BlockSpec((1, 1, n, dh), lambda bi, hi, i: (bi, hi, 0, 0)),
            pl.BlockSpec((1, NA_WIN_R, GRID_W, NA_WIN_R * GRID_W), lambda bi, hi, i: (hi, 0, 0, 0)),
        ],
        out_specs=pl.BlockSpec((1, 1, tq, dh), lambda bi, hi, i: (bi, hi, i, 0)),
        compiler_params=_cparams("arbitrary", "arbitrary", "arbitrary"),
        name="na",
    )(q, k, v, tab)


def _na_bias_table(rpb):
    c = jnp.arange(NA_WIN_R)
    nrow = jnp.arange(NA_WIN_R)
    col = jnp.arange(GRID_W)
    dr = nrow[None, :] - c[:, None] + (NA_WIN_R - 1)
    dc = jnp.clip(col[None, :] - col[:, None], 1 - NA_WIN_C, NA_WIN_C - 1) + (NA_WIN_C - 1)
    col_start = jnp.clip(col - NA_WIN_C // 2, 0, GRID_W - NA_WIN_C)
    col_in = (col[None, :] >= col_start[:, None]) & (col[None, :] < col_start[:, None] + NA_WIN_C)
    tab = rpb[:, dr[:, None, :, None], dc[None, :, None, :]].astype(F32)
    tab = jnp.where(col_in[None, None, :, None, :], tab, -jnp.inf)
    return tab.reshape(rpb.shape[0], NA_WIN_R, GRID_W, NA_WIN_R * GRID_W)


def _rwkv_prep_kernel(zs_ref, zp_ref, zn_ref, zl_ref, par_ref, w2_ref, a2_ref, g2_ref, bd_ref,
                      v_ref, kk_ref, r_ref, ld_ref, kd_ref, bb_ref, g_ref, *, n_ctx, n_tok):
    tp = zs_ref.shape[1]
    i = pl.program_id(1)
    lidx = lax.broadcasted_iota(jnp.int32, (tp, 1), 0)
    gidx = lidx + i * tp
    x = zs_ref[0]
    xp = pltpu.roll(x, 1, 0)
    xp = jnp.where(lidx == 0, zp_ref[0, 7:8, :], xp)
    xp = jnp.where((gidx == 0) | (gidx == n_ctx), 0.0, xp)
    xn = pltpu.roll(x, tp - 1, 0)
    xn = jnp.where(lidx == tp - 1, zn_ref[0, 0:1, :], xn)
    xn = jnp.where((gidx == n_ctx - 1) | (gidx == n_tok - 1), 0.0, xn)
    ts = 0.5 * (xp + xn) - x
    w = RWKV_W
    r = x[:, 0:w] + ts[:, 0:w] * par_ref[0:1, :]
    k = x[:, w:2 * w] + ts[:, w:2 * w] * par_ref[1:2, :]
    v = x[:, 2 * w:3 * w] + ts[:, 2 * w:3 * w] * par_ref[2:3, :]
    kk = k * par_ref[7:8, :]
    ss = _dot(kk * kk, bd_ref[...], precision=HI)
    kk = kk * lax.rsqrt(jnp.maximum(ss, 1e-12))
    v_ref[0] = v
    kk_ref[0] = kk
    r_ref[0] = r
    lo = zl_ref[0] + ts[:, 3 * w:3 * w + 256]
    k_a = par_ref[8:9, :]
    for d in range(2):
        base = d * 128
        lw = jnp.tanh(lo[:, base:base + 32]).astype(BF16)
        la = lo[:, base + 32:base + 64].astype(BF16)
        lg = _sigmoid(lo[:, base + 64:base + 128]).astype(BF16)
        wl = par_ref[3 + d:4 + d, :] + _dot(lw, w2_ref[d])
        softplus_neg = jnp.maximum(-wl, 0.0) + jnp.log(1.0 + jnp.exp(-jnp.abs(wl)))
        ld_ref[d, 0] = -jnp.exp(-softplus_neg - 0.5)
        a = _sigmoid(par_ref[5 + d:6 + d, :] + _dot(la, a2_ref[d]))
        kd_ref[d, 0] = k * (1.0 + (a - 1.0) * k_a)
        bb_ref[d, 0] = kk * a
        g_ref[d, 0] = _dot(lg, g2_ref[d])


def _rwkv_prep(zs, zl, par, w2, a2, g2, bd, n_ctx):
    b, n, _ = zs.shape
    tp = _pick(n, (1280, 256, 128))
    nb8 = n // 8
    sh = jax.ShapeDtypeStruct((b, n, RWKV_W), F32)
    dsh = jax.ShapeDtypeStruct((2, b, n, RWKV_W), F32)
    tspec = lambda w: pl.BlockSpec((1, tp, w), lambda bi, i: (bi, i, 0))
    dspec = pl.BlockSpec((2, 1, tp, RWKV_W), lambda bi, i: (0, bi, i, 0))
    return pl.pallas_call(
        functools.partial(_rwkv_prep_kernel, n_ctx=n_ctx, n_tok=n),
        out_shape=[sh, sh, sh, dsh, dsh, dsh, dsh],
        grid=(b, n // tp),
        in_specs=[
            tspec(1024),
            pl.BlockSpec((1, 8, 1024), lambda bi, i: (bi, jnp.maximum(i * (tp // 8) - 1, 0), 0)),
            pl.BlockSpec((1, 8, 1024), lambda bi, i: (bi, jnp.minimum((i + 1) * (tp // 8), nb8 - 1), 0)),
            tspec(256),
            _const_spec(par.shape), _const_spec(w2.shape), _const_spec(a2.shape), _const_spec(g2.shape),
            _const_spec(bd.shape),
        ],
        out_specs=[tspec(RWKV_W), tspec(RWKV_W), tspec(RWKV_W), dspec, dspec, dspec, dspec],
        compiler_params=_cparams("arbitrary", "arbitrary"),
        name="rwkv_prep",
    )(zs, zs, zs, zl, par, w2, a2, g2, bd)


def _rwkv_scan_kernel(ld_ref, kd_ref, bb_ref, g_ref, v_ref, kk_ref, r_ref, par_ref, y_ref, s_sc):
    c = SCAN_CHUNK
    fwd = pl.program_id(0) == 0

    @pl.when(pl.program_id(2) == 0)
    def _():
        s_sc[...] = jnp.zeros(s_sc.shape, F32)

    row_i = lax.broadcasted_iota(jnp.int32, (c, c), 0)
    col_i = lax.broadcasted_iota(jnp.int32, (c, c), 1)
    t_i = jnp.where(fwd, row_i, col_i)
    s_i = jnp.where(fwd, col_i, row_i)
    incl = s_i <= t_i
    strict = s_i < t_i
    eye = (row_i == col_i).astype(F32)

    ld = ld_ref[0, 0]
    cum = _dot(incl.astype(F32), ld, precision=HI)
    e_tot = jnp.exp(jnp.sum(ld, axis=0, keepdims=True))
    e_neg = jnp.exp(-cum)
    kap_all = kk_ref[0] * jnp.exp(cum - ld)
    khat_all = kd_ref[0, 0] * e_neg
    bhat_all = bb_ref[0, 0] * e_neg
    rhat_all = r_ref[0] * jnp.exp(cum)
    v_all = v_ref[0]
    rk_all = r_ref[0] * kd_ref[0, 0] * par_ref[0:1, :]
    g_all = g_ref[0, 0]

    for h in range(RWKV_HEADS):
        sl = slice(h * RWKV_N, (h + 1) * RWKV_N)
        kap, rhat, khat, bhat, vh = kap_all[:, sl], rhat_all[:, sl], khat_all[:, sl], bhat_all[:, sl], v_all[:, sl]
        l_b = jnp.where(strict, _dot_nt(kap, bhat, HI), 0.0)
        l_k = jnp.where(strict, _dot_nt(kap, khat, HI), 0.0)
        m_b = jnp.where(incl, _dot_nt(rhat, bhat, HI), 0.0)
        m_k = jnp.where(incl, _dot_nt(rhat, khat, HI), 0.0)
        pw = -l_b
        tinv = eye + pw
        for _ in range(5):
            pw = _dot(pw, pw, precision=HI)
            tinv = tinv + _dot(tinv, pw, precision=HI)
        s0 = s_sc[h]
        u = _dot(tinv, _dot_nt(kap, s0, HI) + _dot(l_k, vh, precision=HI), precision=HI)
        y = _dot_nt(rhat, s0, HI) + _dot(m_k, vh, precision=HI) - _dot(m_b, u, precision=HI)
        s_sc[h] = (s0 + _dot_tn(vh, khat, HI) - _dot_tn(u, bhat, HI)) * e_tot[:, sl]
        mu = jnp.mean(y, axis=-1, keepdims=True)
        var = jnp.mean(jnp.square(y - mu), axis=-1, keepdims=True)
        on = (y - mu) * lax.rsqrt(var + RWKV_GN_EPS) * par_ref[1:2, sl] + par_ref[2:3, sl]
        bonus = jnp.sum(rk_all[:, sl], axis=-1, keepdims=True) * vh
        y_ref[0, 0, h] = (on + bonus) * g_all[:, sl]


def _rwkv_scan(ld, kd, bb, g, v, kk, r, par, n_ctx):
    _, b, n, w = ld.shape
    c = SCAN_CHUNK
    nc = n // c
    nc_ctx = n_ctx // c
    assert n % c == 0 and n_ctx % c == 0

    def cidx(d, ci):
        rev = jnp.where(ci < nc_ctx, nc_ctx - 1 - ci, nc - 1 - (ci - nc_ctx))
        return jnp.where(d == 0, ci, rev)

    dspec = pl.BlockSpec((1, 1, c, w), lambda d, bi, ci: (d, bi, cidx(d, ci), 0))
    sspec = pl.BlockSpec((1, c, w), lambda d, bi, ci: (bi, cidx(d, ci), 0))
    return pl.pallas_call(
        _rwkv_scan_kernel,
        out_shape=jax.ShapeDtypeStruct((2, b, RWKV_HEADS, n, RWKV_N), F32),
        grid=(2, b, nc),
        in_specs=[dspec, dspec, dspec, dspec, sspec, sspec, sspec, _const_spec(par.shape)],
        out_specs=pl.BlockSpec((1, 1, RWKV_HEADS, c, RWKV_N), lambda d, bi, ci: (d, bi, 0, cidx(d, ci), 0)),
        scratch_shapes=[pltpu.VMEM((RWKV_HEADS, RWKV_N, RWKV_N), F32)],
        compiler_params=_cparams("arbitrary", "arbitrary", "arbitrary"),
        name="rwkv_scan",
    )(ld, kd, bb, g, v, kk, r, par)


def _merge_kernel(x_ref, mods_ref, gain_ref, ya_ref, yb_ref, yr_ref, ym_ref, gate_ref, wb_ref, wo_ref,
                  o_ref, *, n_ctx, tok_off):
    tm = x_ref.shape[1]
    is_ctx = _is_ctx_rows(tm, pl.program_id(1), tok_off, n_ctx)

    def lift(head_fn, n_heads, w_off):
        acc = None
        for j in range(n_heads):
            t = _dot(head_fn(j), wb_ref[w_off + j * HEAD_DIM:w_off + (j + 1) * HEAD_DIM, :])
            acc = t if acc is None else acc + t
        return acc

    d = x_ref.shape[2]
    merged = _sigmoid(gate_ref[0, :, 0:d]) * lift(lambda j: ya_ref[0, j], GQA_HEADS, 0)
    merged = merged + _sigmoid(gate_ref[0, :, d:2 * d]) * lift(lambda j: yb_ref[0, j], NA_HEADS, GQA_W)
    merged = merged + _sigmoid(gate_ref[0, :, 2 * d:3 * d]) * lift(
        lambda j: (yr_ref[0, 0, j] + yr_ref[1, 0, j]).astype(BF16), RWKV_HEADS, GQA_W + NA_W)
    merged = merged + _sigmoid(gate_ref[0, :, 3 * d:4 * d]) * lift(
        lambda j: ym_ref[0, j], MLA_HEADS, GQA_W + NA_W + RWKV_W)
    y = _dot(merged.astype(BF16), wo_ref[...])
    o_ref[0] = x_ref[0] + _seg_mod(mods_ref, 2, is_ctx) * _rms(y, gain_ref[...])


def _merge(x, mods, gain, ya, yb, yr, ym, gate, wb, wo, *, n_ctx, tok_off, n_out):
    b, n, d = x.shape
    tm = _pick(n_out, (256, 128))
    assert tok_off % tm == 0
    to = tok_off // tm
    xspec = lambda w: pl.BlockSpec((1, tm, w), lambda bi, i: (bi, i + to, 0))
    yspec = lambda nh: pl.BlockSpec((1, nh, tm, HEAD_DIM), lambda bi, i: (bi, 0, i, 0))
    return pl.pallas_call(
        functools.partial(_merge_kernel, n_ctx=n_ctx, tok_off=tok_off),
        out_shape=jax.ShapeDtypeStruct((b, n_out, d), F32),
        grid=(b, n_out // tm),
        in_specs=[
            xspec(d),
            pl.BlockSpec((1, 16, d), lambda bi, i: (bi, 0, 0)),
            _const_spec((1, d)),
            yspec(GQA_HEADS), yspec(NA_HEADS),
            pl.BlockSpec((2, 1, RWKV_HEADS, tm, RWKV_N), lambda bi, i: (0, bi, 0, i + to, 0)),
            yspec(MLA_HEADS),
            xspec(4 * d),
            _const_spec(wb.shape), _const_spec(wo.shape),
        ],
        out_specs=pl.BlockSpec((1, tm, d), lambda bi, i: (bi, i, 0)),
        compiler_params=_cparams("arbitrary", "arbitrary"),
        name="merge",
    )(x, mods, gain, ya, yb, yr, ym, gate, wb, wo)


def _ffn_kernel(x_ref, mods_ref, gpre_ref, gpost_ref, wg_ref, wu_ref, wd_ref, o_ref, *, n_ctx, tok_off):
    tm = x_ref.shape[1]
    is_ctx = _is_ctx_rows(tm, pl.program_id(1), tok_off, n_ctx)
    x = x_ref[0]
    h = _rms(x, gpre_ref[...]) * (1.0 + _seg_mod(mods_ref, 4, is_ctx)) + _seg_mod(mods_ref, 3, is_ctx)
    hb = h.astype(BF16)
    a = _silu(_dot(hb, wg_ref[...])) * _dot(hb, wu_ref[...])
    y = _dot(a.astype(BF16), wd_ref[...])
    o_ref[0] = x + _seg_mod(mods_ref, 5, is_ctx) * _rms(y, gpost_ref[...])


def _ffn(x, mods, gpre, gpost, wg, wu, wd, *, n_ctx, tok_off):
    b, n, d = x.shape
    tm = _pick(n, (256, 128))
    tspec = pl.BlockSpec((1, tm, d), lambda bi, i: (bi, i, 0))
    return pl.pallas_call(
        functools.partial(_ffn_kernel, n_ctx=n_ctx, tok_off=tok_off),
        out_shape=jax.ShapeDtypeStruct((b, n, d), F32),
        grid=(b, n // tm),
        in_specs=[
            tspec,
            pl.BlockSpec((1, 16, d), lambda bi, i: (bi, 0, 0)),
            _const_spec((1, d)), _const_spec((1, d)),
            _const_spec(wg.shape, True), _const_spec(wu.shape, True), _const_spec(wd.shape, True),
        ],
        out_specs=tspec,
        compiler_params=_cparams("arbitrary", "arbitrary"),
        name="ffn",
    )(x, mods, gpre, gpost, wg, wu, wd)


def _router_kernel(x_ref, mods_ref, gpre_ref, wr_ref, br_ref, h_ref, comb_ref, *, n_ctx, tok_off):
    tm = x_ref.shape[1]
    is_ctx = _is_ctx_rows(tm, pl.program_id(1), tok_off, n_ctx)
    h = _rms(x_ref[0], gpre_ref[...]) * (1.0 + _seg_mod(mods_ref, 4, is_ctx)) + _seg_mod(mods_ref, 3, is_ctx)
    hb = h.astype(BF16)
    h_ref[0] = hb
    lane = lax.broadcasted_iota(jnp.int32, (tm, LANES), 1)
    valid = lane < N_EXPERTS
    logits = jnp.where(valid, _dot(hb, wr_ref[...]) + br_ref[...], -jnp.inf)
    e = jnp.exp(logits - jnp.max(logits, axis=-1, keepdims=True))
    p = jnp.where(valid, e / jnp.sum(e, axis=-1, keepdims=True), -1.0)
    p1 = jnp.max(p, axis=-1, keepdims=True)
    i1 = jnp.min(jnp.where(p == p1, lane, LANES), axis=-1, keepdims=True)
    rest = jnp.where(lane == i1, -1.0, p)
    p2 = jnp.max(rest, axis=-1, keepdims=True)
    i2 = jnp.min(jnp.where(rest == p2, lane, LANES), axis=-1, keepdims=True)
    den = p1 + p2
    comb_ref[0] = jnp.where(lane == i1, p1 / den, 0.0) + jnp.where(lane == i2, p2 / den, 0.0)


def _router(x, mods, gpre, wr, br, *, n_ctx, tok_off):
    b, n, d = x.shape
    tm = _pick(n, (256, 128))
    tspec = lambda w: pl.BlockSpec((1, tm, w), lambda bi, i: (bi, i, 0))
    return pl.pallas_call(
        functools.partial(_router_kernel, n_ctx=n_ctx, tok_off=tok_off),
        out_shape=[jax.ShapeDtypeStruct((b, n, d), BF16), jax.ShapeDtypeStruct((b, n, LANES), F32)],
        grid=(b, n // tm),
        in_specs=[
            tspec(d),
            pl.BlockSpec((1, 16, d), lambda bi, i: (bi, 0, 0)),
            _const_spec((1, d)), _const_spec(wr.shape), _const_spec(br.shape),
        ],
        out_specs=[tspec(d), tspec(LANES)],
        compiler_params=_cparams("arbitrary", "arbitrary"),
        name="moe_router",
    )(x, mods, gpre, wr, br)


def _experts_kernel(h_ref, comb_ref, x_ref, mods_ref, gpost_ref, wg_ref, wu_ref, wd_ref, o_ref, acc_sc,
                    *, n_ctx, tok_off):
    tm = x_ref.shape[1]
    e = pl.program_id(2)

    @pl.when(e == 0)
    def _():
        acc_sc[...] = jnp.zeros(acc_sc.shape, F32)

    hb = h_ref[0]
    a = _silu(_dot(hb, wg_ref[0])) * _dot(hb, wu_ref[0])
    y = _dot(a.astype(BF16), wd_ref[0])
    lane = lax.broadcasted_iota(jnp.int32, (tm, LANES), 1)
    ce = jnp.sum(jnp.where(lane == e, comb_ref[0], 0.0), axis=-1, keepdims=True)
    acc_sc[...] += ce * y

    @pl.when(e == N_EXPERTS - 1)
    def _():
        is_ctx = _is_ctx_rows(tm, pl.program_id(1), tok_off, n_ctx)
        o_ref[0] = x_ref[0] + _seg_mod(mods_ref, 5, is_ctx) * _rms(acc_sc[...], gpost_ref[...])


def _experts(h, comb, x, mods, gpost, wg, wu, wd, *, n_ctx, tok_off):
    b, n, d = x.shape
    tm = _pick(n, (512, 256, 128))
    ff = wg.shape[-1]
    tspec = lambda w: pl.BlockSpec((1, tm, w), lambda bi, i, e: (bi, i, 0))
    return pl.pallas_call(
        functools.partial(_experts_kernel, n_ctx=n_ctx, tok_off=tok_off),
        out_shape=jax.ShapeDtypeStruct((b, n, d), F32),
        grid=(b, n // tm, N_EXPERTS),
        in_specs=[
            tspec(d), tspec(LANES), tspec(d),
            pl.BlockSpec((1, 16, d), lambda bi, i, e: (bi, 0, 0)),
            _const_spec((1, d)),
            pl.BlockSpec((1, d, ff), lambda bi, i, e: (e, 0, 0)),
            pl.BlockSpec((1, d, ff), lambda bi, i, e: (e, 0, 0)),
            pl.BlockSpec((1, ff, d), lambda bi, i, e: (e, 0, 0)),
        ],
        out_specs=tspec(d),
        scratch_shapes=[pltpu.VMEM((tm, d), F32)],
        compiler_params=_cparams("arbitrary", "arbitrary", "arbitrary"),
        name="moe_experts",
    )(h, comb, x, mods, gpost, wg, wu, wd)


def _rope_tables(n_ctx, n_lat, rot_dim):
    t = jnp.arange(n_lat)
    row = (t // GRID_W).astype(F32)
    col = (t % GRID_W).astype(F32)
    n_pairs = rot_dim // 4
    inv_freq = ROPE_THETA ** (-jnp.arange(n_pairs, dtype=F32) / n_pairs)
    ang = jnp.concatenate([row[:, None] * inv_freq, col[:, None] * inv_freq], axis=-1)
    cos, sin = jnp.cos(ang), jnp.sin(ang)
    cosf = jnp.concatenate([cos, cos], axis=-1)
    sinf = jnp.concatenate([-sin, sin], axis=-1)
    cosf = jnp.concatenate([jnp.ones((n_ctx, rot_dim), F32), cosf], axis=0)
    sinf = jnp.concatenate([jnp.zeros((n_ctx, rot_dim), F32), sinf], axis=0)
    return cosf, sinf


def _pack_inproj_weight(w_in, mu_x, w1, a1, g1):
    splits = (GQA_W, GQA_KV_W, GQA_KV_W, NA_W, NA_W, NA_W, RWKV_W, RWKV_W, RWKV_W,
              MLA_Q_LORA, MLA_KV_LORA, MLA_ROPE, 4 * w_in.shape[0])
    parts, o = [], 0
    for s in splits:
        parts.append(w_in[:, o:o + s])
        o += s
    gq, gk, gv, nq, nk, nv, rr, rk, rv, cq, ckv, kr, gate = parts
    lora_plain = jnp.concatenate([jnp.concatenate([w1[d], a1[d], g1[d]], axis=1) for d in range(2)], axis=1)
    lora_shift = jnp.concatenate(
        [jnp.concatenate([mu_x[0][:, None] * w1[d], mu_x[1][:, None] * a1[d], mu_x[2][:, None] * g1[d]], axis=1)
         for d in range(2)], axis=1)
    kr_pad = jnp.concatenate(
        [kr, _swap_halves_cols(kr, MLA_ROPE), jnp.zeros((kr.shape[0], LANES - 2 * MLA_ROPE), kr.dtype)], axis=1)
    cols = {
        "gq": gq, "gqs": _swap_halves_cols(gq, HEAD_DIM), "gk": gk, "gks": _swap_halves_cols(gk, HEAD_DIM),
        "gv": gv, "nq": nq, "nk": nk, "nv": nv,
        "zs": jnp.concatenate([rr, rk, rv, lora_shift], axis=1), "zl": lora_plain,
        "cq": cq, "ckv": ckv, "kr": kr_pad, "gate": gate,
    }
    for name, w in _IN_GROUPS:
        assert cols[name].shape[1] == w, (name, cols[name].shape)
    return jnp.concatenate([cols[name] for name, _ in _IN_GROUPS], axis=1).astype(BF16)


def kernel(x, c, ctx, c_ctx, w_mod, b_mod, norm_mix_pre, norm_mix_post, norm_ffn_pre, norm_ffn_post, w_in, gqa_q_norm, gqa_k_norm, na_rpb, rwkv_mu_x, rwkv_mu_p, rwkv_w0, rwkv_w1, rwkv_w2, rwkv_a0, rwkv_a1, rwkv_a2, rwkv_g1, rwkv_g2, rwkv_k_k, rwkv_k_a, rwkv_r_k, rwkv_ln_w, rwkv_ln_b, mla_q_norm, mla_w_uq, mla_kv_norm, mla_w_ukv, w_branch, w_out, ffn_w_gate, ffn_w_up, ffn_w_down, moe_w_router, moe_b_router, moe_w_gate, moe_w_up, moe_w_down):
    bsz, n_lat, d = x.shape
    n_ctx = ctx.shape[1]
    n = n_ctx + n_lat
    depth = w_mod.shape[0]
    assert n_lat % GRID_W == 0 and d == 1024

    c8 = jnp.zeros((8, d), F32).at[:bsz].set(c).at[bsz].set(c_ctx)
    mod_all = _adaln(c8, w_mod, b_mod)

    rope = _rope_tables(n_ctx, n_lat, HEAD_DIM) + _rope_tables(n_ctx, n_lat, MLA_ROPE)
    bd = jnp.kron(jnp.eye(RWKV_HEADS, dtype=F32), jnp.ones((RWKV_N, RWKV_N), F32))

    xs = jnp.concatenate([ctx, x], axis=1)
    for l in range(depth):
        last = l == depth - 1
        m6 = mod_all[l].reshape(8, 6, d)
        mods = jnp.zeros((bsz, 16, d), F32).at[:, 0:6].set(m6[:bsz]).at[:, 8:14].set(m6[bsz][None])

        w_all = _pack_inproj_weight(w_in[l], rwkv_mu_x[l], rwkv_w1[l], rwkv_a1[l], rwkv_g1[l])
        gq_g, gk_g = gqa_q_norm[l], gqa_k_norm[l]
        hgains = jnp.zeros((8, HEAD_DIM), F32).at[0].set(gq_g).at[1].set(_swap_halves_cols(gq_g, HEAD_DIM))
        hgains = hgains.at[2].set(gk_g).at[3].set(_swap_halves_cols(gk_g, HEAD_DIM))
        uq = mla_w_uq[l].reshape(MLA_Q_LORA, MLA_HEADS, MLA_NOPE + MLA_ROPE)
        uq_r = uq[:, :, MLA_NOPE:].reshape(MLA_Q_LORA, MLA_HEADS * MLA_ROPE)
        wuq = jnp.concatenate([uq[:, :, :MLA_NOPE].reshape(MLA_Q_LORA, MLA_HEADS * MLA_NOPE), uq_r,
                               _swap_halves_cols(uq_r, MLA_ROPE)], axis=1).astype(BF16)
        ukv = mla_w_ukv[l].reshape(MLA_KV_LORA, MLA_HEADS, MLA_NOPE + MLA_V)
        wukv = jnp.concatenate([ukv[:, :, :MLA_NOPE].reshape(MLA_KV_LORA, -1),
                                ukv[:, :, MLA_NOPE:].reshape(MLA_KV_LORA, -1)], axis=1).astype(BF16)
        mgains = jnp.zeros((8, MLA_Q_LORA), F32).at[0].set(mla_q_norm[l]).at[1, :MLA_KV_LORA].set(mla_kv_norm[l])

        (gq, gk, gv, nq, nk, nv, zs, zl, mqn, mqr, mkn, mkr, mv, gate) = _inproj(
            xs, mods, norm_mix_pre[l][None], w_all, hgains, wuq, wukv, mgains, rope, n_ctx)

        tk = _pick(n, (640, 256, 128))
        mq = jnp.concatenate([mqn, mqr], axis=-1)
        mk = jnp.concatenate([mkn, jnp.broadcast_to(mkr[:, None], mkn.shape[:-1] + (MLA_ROPE,))], axis=-1)
        ya = _flash(gq, gk, gv, q_off=n_ctx, n_q=n_lat, n_kv=n, tq=128, tk=tk)
        ym = _flash(mq, mk, mv, q_off=n_ctx, n_q=n_lat, n_kv=n, tq=256, tk=tk)
        yb = _na(nq, nk, nv, _na_bias_table(na_rpb[l]), n_ctx)
        if not last:
            tc = _pick(n_ctx, (256, 128))
            yca = _flash(gq, gk, gv, q_off=0, n_q=n_ctx, n_kv=n_ctx, tq=tc, tk=tc)
            ycm = _flash(mq, mk, mv, q_off=0, n_q=n_ctx, n_kv=n_ctx, tq=tc, tk=tc)
            ycb = _flash(nq, nk, nv, q_off=0, n_q=n_ctx, n_kv=n_ctx, tq=tc, tk=tc)
            ya = jnp.concatenate([yca, ya], axis=2)
            ym = jnp.concatenate([ycm, ym], axis=2)
            yb = jnp.concatenate([ycb, yb], axis=2)

        par = jnp.zeros((16, RWKV_W), F32).at[0:3].set(rwkv_mu_p[l]).at[3:5].set(rwkv_w0[l]).at[5:7].set(rwkv_a0[l])
        par = par.at[7].set(rwkv_k_k[l]).at[8].set(rwkv_k_a[l])
        v_r, kk_r, r_r, ld, kd, bb, g_r = _rwkv_prep(
            zs, zl, par, rwkv_w2[l].astype(BF16), rwkv_a2[l].astype(BF16), rwkv_g2[l].astype(BF16), bd, n_ctx)
        par2 = jnp.zeros((8, RWKV_W), F32).at[0].set(rwkv_r_k[l].reshape(-1)).at[1].set(rwkv_ln_w[l])
        par2 = par2.at[2].set(rwkv_ln_b[l])
        yr = _rwkv_scan(ld, kd, bb, g_r, v_r, kk_r, r_r, par2, n_ctx)

        tok_off = n_ctx if last else 0
        n_out = n_lat if last else n
        xs2 = _merge(xs, mods, norm_mix_post[l][None], ya, yb, yr, ym, gate,
                     w_branch[l].astype(BF16), w_out[l].astype(BF16), n_ctx=n_ctx, tok_off=tok_off, n_out=n_out)

        i = l // 2
        if l % 2 == 0:
            xs = _ffn(xs2, mods, norm_ffn_pre[l][None], norm_ffn_post[l][None],
                      ffn_w_gate[i].astype(BF16), ffn_w_up[i].astype(BF16), ffn_w_down[i].astype(BF16),
                      n_ctx=n_ctx, tok_off=tok_off)
        else:
            wr = jnp.zeros((d, LANES), F32).at[:, :N_EXPERTS].set(moe_w_router[i]).astype(BF16)
            br = jnp.zeros((1, LANES), F32).at[0, :N_EXPERTS].set(moe_b_router[i])
            hb, comb = _router(xs2, mods, norm_ffn_pre[l][None], wr, br, n_ctx=n_ctx, tok_off=tok_off)
            xs = _experts(hb, comb, xs2, mods, norm_ffn_post[l][None],
                          moe_w_gate[i].astype(BF16), moe_w_up[i].astype(BF16), moe_w_down[i].astype(BF16),
                          n_ctx=n_ctx, tok_off=tok_off)
    return xs
```
